```python
import jax, jax.numpy as jnp
from jax import lax
import numpy as np

D_MODEL = 1024
BATCH = 16
SEQ = 2048
DEPTH = 1

CHUNK = 64

N_HEADS_A = 16
HEAD_A = D_MODEL // N_HEADS_A
WIDTH_A = N_HEADS_A * HEAD_A
D_DECAY_LORA = 64
D_AAA_LORA = 64
D_GATE_LORA = 160
GN_EPS = 64e-5
SPLITS_A = (WIDTH_A, WIDTH_A, WIDTH_A, D_DECAY_LORA, D_AAA_LORA, D_GATE_LORA)
COLS_A = sum(SPLITS_A)

GMLP_CHUNK = 2 * CHUNK
N_GROUPS_B = 8
GROUP_B = 128
WIDTH_B = N_GROUPS_B * GROUP_B
LN_EPS = 1e-5

N_BRANCHES = 2
IN_COLS = COLS_A + 2 * WIDTH_B + N_BRANCHES * D_MODEL

N_EXPERTS = 32
TOP_K = 4
D_FF_EXPERT = D_MODEL
SWIGLU_LIMIT = 7.0
SWIGLU_ALPHA = 1.702
MOE_BLOCK = 128

RMS_EPS = 1e-5

kernel_name = "hybrid_rwkv7_gmlp_gated_moe_block"


def rms_norm(x, g):
    xf = x.astype(jnp.float32)
    y = xf * lax.rsqrt(jnp.mean(xf * xf, axis=-1, keepdims=True) + RMS_EPS)
    return (y * g.astype(jnp.float32)).astype(x.dtype)


def token_shift(z):
    return jnp.pad(z[:, :-1], ((0, 0), (1, 0), (0, 0)))


def rwkv7_time_mix(z, time_mix, w_decay_up, w0, a_up, a0, g_up, k_k, k_a, r_k, gn_w, gn_b):
    B, S, _ = z.shape
    f32 = jnp.float32
    H, N = N_HEADS_A, HEAD_A
    z = z + (token_shift(z) - z) * time_mix
    idx = [int(i) for i in np.cumsum(SPLITS_A)[:-1]]
    r, k, v, zw, za, zg = jnp.split(z, idx, axis=-1)
    w_log = -jax.nn.softplus(-(w0 + jnp.tanh(zw) @ w_decay_up).astype(f32)) - 0.5
    decay = jnp.exp(-jnp.exp(w_log))
    a = jax.nn.sigmoid((a0 + za @ a_up).astype(f32))
    g = jax.nn.sigmoid(zg) @ g_up

    def heads(t):
        return t.astype(f32).reshape(B, S, H, N)

    r, k, v, decay, a = heads(r), heads(k), heads(v), heads(decay), heads(a)
    kk = k * k_k.astype(f32).reshape(H, N)
    kk = kk / jnp.maximum(jnp.sqrt(jnp.sum(kk * kk, axis=-1, keepdims=True)), 1e-12)
    k = k * (1.0 + (a - 1.0) * k_a.astype(f32).reshape(H, N))

    def step(state, inp):
        r_t, k_t, v_t, w_t, kk_t, a_t = inp
        sa = jnp.einsum('bhij,bhj->bhi', state, -kk_t)
        state = (state * w_t[:, :, None, :]
                 + sa[..., None] * (kk_t * a_t)[:, :, None, :]
                 + v_t[..., None] * k_t[:, :, None, :])
        y_t = jnp.einsum('bhij,bhj->bhi', state, r_t)
        return state, y_t

    xs = tuple(jnp.moveaxis(t, 1, 0) for t in (r, k, v, decay, kk, a))
    _, y = lax.scan(step, jnp.zeros((B, H, N, N), f32), xs)
    y = jnp.moveaxis(y, 0, 1)
    mean = jnp.mean(y, axis=-1, keepdims=True)
    var = jnp.mean(jnp.square(y - mean), axis=-1, keepdims=True)
    y = ((y - mean) * lax.rsqrt(var + GN_EPS) * gn_w.astype(f32).reshape(H, N)
         + gn_b.astype(f32).reshape(H, N))
    y = y + jnp.sum(r * k * r_k.astype(f32).reshape(H, N), axis=-1, keepdims=True) * v
    y = y.reshape(B, S, WIDTH_A) * g.astype(f32)
    return y.astype(z.dtype)


def gmlp_spatial_gate(u_raw, v_raw, ln_v_g, ln_v_b, w_spatial, b_spatial):
    B, S, _ = u_raw.shape
    f32 = jnp.float32
    u = jax.nn.gelu(u_raw.astype(f32), approximate=False)
    vf = jax.nn.gelu(v_raw.astype(f32), approximate=False)
    mean = jnp.mean(vf, axis=-1, keepdims=True)
    var = jnp.mean(jnp.square(vf - mean), axis=-1, keepdims=True)
    vn = (vf - mean) * lax.rsqrt(var + LN_EPS) * ln_v_g.astype(f32) + ln_v_b.astype(f32)
    vc = vn.reshape(B, S // GMLP_CHUNK, GMLP_CHUNK, N_GROUPS_B, GROUP_B)
    mask = jnp.tril(jnp.ones((GMLP_CHUNK, GMLP_CHUNK), dtype=bool))
    ws = jnp.where(mask[None], w_spatial.astype(f32), 0.0)
    s = jnp.einsum('gij,bcjgd->bcigd', ws, vc) + b_spatial.astype(f32).T[:, :, None]
    return (u * s.reshape(B, S, WIDTH_B)).astype(u_raw.dtype)


def moe_ffn(h, w_router, b_router, w_gate_up, b_gate_up, w_down, b_down):
    B, S, D = h.shape
    T = B * S
    n_assign = T * TOP_K
    n_blocks = n_assign // MOE_BLOCK + N_EXPERTS
    xt = h.reshape(T, D)
    logits = (xt @ w_router + b_router).astype(jnp.float32)
    top_v, top_i = lax.top_k(logits, TOP_K)
    gates = jax.nn.softmax(top_v, axis=-1)
    flat_e = top_i.reshape(-1).astype(jnp.int32)
    flat_tok = jnp.arange(n_assign, dtype=jnp.int32) // TOP_K
    order = jnp.argsort(flat_e, stable=True)
    sorted_e = flat_e[order]
    counts = jnp.bincount(flat_e, length=N_EXPERTS).astype(jnp.int32)
    padded = (counts + MOE_BLOCK - 1) // MOE_BLOCK * MOE_BLOCK
    pad_end = jnp.cumsum(padded)
    pad_start = pad_end - padded
    grp_start = jnp.cumsum(counts) - counts
    dest = pad_start[sorted_e] + jnp.arange(n_assign, dtype=jnp.int32) - grp_start[sorted_e]
    slot_tok = jnp.zeros((n_blocks * MOE_BLOCK,), jnp.int32).at[dest].set(flat_tok[order])
    block_start = jnp.arange(n_blocks, dtype=jnp.int32) * MOE_BLOCK
    block_expert = jnp.minimum(jnp.searchsorted(pad_end, block_start, side='right'),
                               N_EXPERTS - 1).astype(jnp.int32)

    def expert_block(args):
        tok, e = args
        hgu = xt[tok] @ w_gate_up[e] + b_gate_up[e]
        gate = jnp.minimum(hgu[:, 0::2], SWIGLU_LIMIT)
        up = jnp.clip(hgu[:, 1::2], -SWIGLU_LIMIT, SWIGLU_LIMIT)
        act = gate * jax.nn.sigmoid(SWIGLU_ALPHA * gate) * (up + 1.0)
        return act @ w_down[e] + b_down[e]

    y_slots = lax.map(expert_block, (slot_tok.reshape(n_blocks, MOE_BLOCK), block_expert))
    y_slots = y_slots.reshape(n_blocks * MOE_BLOCK, D)
    slot_of_assign = jnp.zeros((n_assign,), jnp.int32).at[order].set(dest)
    y_sel = y_slots[slot_of_assign].reshape(T, TOP_K, D)
    out = jnp.einsum('tk,tkd->td', gates.astype(h.dtype), y_sel)
    return out.reshape(B, S, D)


def setup_inputs(seed: int = 0) -> dict:
    key = jax.random.key(seed)
    ks = jax.random.split(key, 32)
    L, D = DEPTH, D_MODEL
    f32 = jnp.float32

    def nrm(k, shape, scale):
        return jax.random.normal(k, shape, f32) * scale

    return {
        "x": jax.random.normal(ks[0], (BATCH, SEQ, D), f32),
        "norm1_g": 1.0 + nrm(ks[1], (L, D), 0.05),
        "w_in": nrm(ks[2], (L, D, IN_COLS), D ** -0.5),
        "b_gate": nrm(ks[3], (L, N_BRANCHES * D), 0.1),
        "time_mix": jax.random.uniform(ks[4], (L, COLS_A), f32),
        "w_decay_up": nrm(ks[5], (L, D_DECAY_LORA, WIDTH_A), D_DECAY_LORA ** -0.5),
        "w0": jax.random.uniform(ks[6], (L, WIDTH_A), f32, minval=-5.0, maxval=-1.0),
        "a_up": nrm(ks[7], (L, D_AAA_LORA, WIDTH_A), D_AAA_LORA ** -0.5),
        "a0": nrm(ks[8], (L, WIDTH_A), 0.1),
        "g_up": nrm(ks[9], (L, D_GATE_LORA, WIDTH_A), D_GATE_LORA ** -0.5),
        "k_k": 0.85 + nrm(ks[10], (L, WIDTH_A), 0.05),
        "k_a": 1.0 + nrm(ks[11], (L, WIDTH_A), 0.05),
        "r_k": nrm(ks[12], (L, WIDTH_A), 0.1),
        "gn_w": 1.0 + nrm(ks[13], (L, WIDTH_A), 0.05),
        "gn_b": nrm(ks[14], (L, WIDTH_A), 0.02),
        "ln_v_g": 1.0 + nrm(ks[15], (L, WIDTH_B), 0.05),
        "ln_v_b": nrm(ks[16], (L, WIDTH_B), 0.02),
        "w_spatial": nrm(ks[17], (L, N_GROUPS_B, GMLP_CHUNK, GMLP_CHUNK), GMLP_CHUNK ** -0.5),
        "b_spatial": 1.0 + nrm(ks[18], (L, N_GROUPS_B, GMLP_CHUNK), 0.1),
        "w_out": nrm(ks[19], (L, D, D), D ** -0.5),
        "norm2_g": 1.0 + nrm(ks[20], (L, D), 0.05),
        "w_router": nrm(ks[21], (L, D, N_EXPERTS), D ** -0.5),
        "b_router": nrm(ks[22], (L, N_EXPERTS), 0.01),
        "w_gate_up": nrm(ks[23], (L, N_EXPERTS, D, 2 * D_FF_EXPERT), D ** -0.5),
        "b_gate_up": nrm(ks[24], (L, N_EXPERTS, 2 * D_FF_EXPERT), 0.02),
        "w_down": nrm(ks[25], (L, N_EXPERTS, D_FF_EXPERT, D), D_FF_EXPERT ** -0.5),
        "b_down": nrm(ks[26], (L, N_EXPERTS, D), 0.02),
        "norm_f_g": 1.0 + nrm(ks[27], (D,), 0.05),
    }


def reference(x, norm1_g, w_in, b_gate, time_mix, w_decay_up, w0, a_up, a0, g_up, k_k, k_a, r_k,
              gn_w, gn_b, ln_v_g, ln_v_b, w_spatial, b_spatial, w_out, norm2_g, w_router, b_router,
              w_gate_up, b_gate_up, w_down, b_down, norm_f_g):
    for l in range(DEPTH):
        h = rms_norm(x, norm1_g[l])
        z = h @ w_in[l]
        z_a, u_raw, v_raw, gate_logits = jnp.split(
            z, [COLS_A, COLS_A + WIDTH_B, COLS_A + 2 * WIDTH_B], axis=-1)
        y_a = rwkv7_time_mix(z_a, time_mix[l], w_decay_up[l], w0[l], a_up[l], a0[l], g_up[l],
                             k_k[l], k_a[l], r_k[l], gn_w[l], gn_b[l])
        y_b = gmlp_spatial_gate(u_raw, v_raw, ln_v_g[l], ln_v_b[l], w_spatial[l], b_spatial[l])
        gates = jax.nn.sigmoid(gate_logits + b_gate[l])
        g_a, g_b = jnp.split(gates, N_BRANCHES, axis=-1)
        x = x + (g_a * y_a + g_b * y_b) @ w_out[l]
        x = x + moe_ffn(rms_norm(x, norm2_g[l]), w_router[l], b_router[l], w_gate_up[l],
                        b_gate_up[l], w_down[l], b_down[l])
    return rms_norm(x, norm_f_g)
```

```python
import functools

import jax
import jax.numpy as jnp
from jax import lax
from jax.experimental import pallas as pl
from jax.experimental.pallas import tpu as pltpu

F32 = jnp.float32
BF16 = jnp.bfloat16

D_MODEL = 1024
N_HEADS = 16
HEAD = 64
D_DECAY_LORA = 64
D_AAA_LORA = 64
D_GATE_LORA = 160
LORA = D_DECAY_LORA + D_AAA_LORA + D_GATE_LORA
LORA_PAD = 512
COLS_A = 3 * D_MODEL + LORA
GN_EPS = 64e-5
LN_EPS = 1e-5
RMS_EPS = 1e-5
GMLP_CHUNK = 128
N_GROUPS = 8
N_EXPERTS = 32
TOP_K = 4
SWIGLU_LIMIT = 7.0
SWIGLU_ALPHA = 1.702

Z_COLS = 7 * D_MODEL + LORA_PAD
LORA_BLOCK = (7 * D_MODEL) // LORA_PAD

RWKV_CHUNK = 64
MOE_ROWS = 256
VMEM_LIMIT = 56 * 1024 * 1024


def _dot(a, b):
    return jnp.dot(a, b, preferred_element_type=F32)


def _dot_nt(a, b):
    return lax.dot_general(a, b, (((1,), (1,)), ((), ())), preferred_element_type=F32)


def _dot_tn(a, b):
    return lax.dot_general(a, b, (((0,), (0,)), ((), ())), preferred_element_type=F32)


def _gelu(x):
    return 0.5 * x * (1.0 + lax.erf(x * (2.0 ** -0.5)))


def _split(x):
    hi = x.astype(BF16)
    lo = (x - hi.astype(F32)).astype(BF16)
    return hi, lo


def _dot3(a, b):
    ah, al = _split(a)
    bh, bl = _split(b)
    return _dot(ah, bh) + _dot(ah, bl) + _dot(al, bh)


def _inproj_kernel(x_ref, g_ref, w_ref, z_ref, h_scr):
    @pl.when(pl.program_id(1) == 0)
    def _():
        x = x_ref[...]
        ms = jnp.mean(x * x, axis=-1, keepdims=True)
        h_scr[...] = (x * lax.rsqrt(ms + RMS_EPS) * g_ref[...]).astype(BF16)

    z_ref[...] = _dot(h_scr[...], w_ref[...])


def _inproj(x2d, g, w_bf16):
    t = x2d.shape[0]
    tm = min(1024, t)
    tn = 1536
    return pl.pallas_call(
        _inproj_kernel,
        grid=(t // tm, Z_COLS // tn),
        in_specs=[
            pl.BlockSpec((tm, D_MODEL), lambda i, j: (i, 0)),
            pl.BlockSpec((1, D_MODEL), lambda i, j: (0, 0)),
            pl.BlockSpec((D_MODEL, tn), lambda i, j: (0, j)),
        ],
        out_specs=pl.BlockSpec((tm, tn), lambda i, j: (i, j)),
        out_shape=jax.ShapeDtypeStruct((t, Z_COLS), F32),
        scratch_shapes=[pltpu.VMEM((tm, D_MODEL), BF16)],
        compiler_params=pltpu.CompilerParams(
            dimension_semantics=("parallel", "arbitrary"), vmem_limit_bytes=VMEM_LIMIT),
        name="inproj",
    )(x2d, g, w_bf16)


def _rwkv_kernel(zr_ref, zk_ref, zv_ref, zl_ref, tmix_ref, tml_ref, vecs_ref, wd_ref, au_ref, gu_ref,
                 bd_ref, y_ref, state_ref, prev_ref):
    c = RWKV_CHUNK

    @pl.when(pl.program_id(1) == 0)
    def _():
        state_ref[...] = jnp.zeros_like(state_ref)
        prev_ref[...] = jnp.zeros_like(prev_ref)

    row = lax.broadcasted_iota(jnp.int32, (c, 1), 0)

    def shift_lerp(z, lo, width, tm):
        prev = prev_ref[:, lo:lo + width]
        sh = jnp.where(row == 0, prev, pltpu.roll(z, 1, 0))
        prev_ref[:, lo:lo + width] = z[c - 1:c, :]
        return z + (sh - z) * tm

    r = shift_lerp(zr_ref[...], 0, D_MODEL, tmix_ref[0:1, :])
    k = shift_lerp(zk_ref[...], D_MODEL, D_MODEL, tmix_ref[1:2, :])
    v = shift_lerp(zv_ref[...], 2 * D_MODEL, D_MODEL, tmix_ref[2:3, :])
    xl = shift_lerp(zl_ref[...], 3 * D_MODEL, LORA_PAD, tml_ref[...])

    w0 = vecs_ref[0:1, :]
    a0 = vecs_ref[1:2, :]
    k_k = vecs_ref[2:3, :]
    k_a = vecs_ref[3:4, :]
    r_k = vecs_ref[4:5, :]
    gn_w = vecs_ref[5:6, :]
    gn_b = vecs_ref[6:7, :]
    bd = bd_ref[...]

    def head_sum(x):
        hi, lo = _split(x)
        return _dot(hi, bd) + _dot(lo, bd)

    w_log = -jax.nn.softplus(-(w0 + _dot(jnp.tanh(xl).astype(BF16), wd_ref[...]))) - 0.5
    logw = -jnp.exp(w_log)
    a = jax.nn.sigmoid(a0 + _dot(xl.astype(BF16), au_ref[...]))
    g = _dot(jax.nn.sigmoid(xl).astype(BF16), gu_ref[...])

    kkr = k * k_k
    kk = kkr / jnp.maximum(jnp.sqrt(head_sum(kkr * kkr)), 1e-12)
    k2 = k * (1.0 + (a - 1.0) * k_a)
    bvec = kk * a

    ri = lax.broadcasted_iota(jnp.int32, (c, c), 0)
    ci = lax.broadcasted_iota(jnp.int32, (c, c), 1)
    strict = ri > ci
    incl = ri >= ci
    tri = jnp.where(incl, 1.0, 0.0).astype(BF16)
    lw_hi, lw_lo = _split(logw)
    lp = _dot(tri, lw_hi) + _dot(tri, lw_lo)
    lp_last = lp[c - 1:c, :]
    e_neg = jnp.exp(-lp)
    e_rem = jnp.exp(lp_last - lp)
    ad = (-kk * jnp.exp(lp - logw)).astype(BF16)
    rd = (r * jnp.exp(lp)).astype(BF16)
    bi = (bvec * e_neg).astype(BF16)
    ki = (k2 * e_neg).astype(BF16)
    bip = (bvec * e_rem).astype(BF16)
    kip = (k2 * e_rem).astype(BF16)
    dec = jnp.exp(lp_last)
    vb = v.astype(BF16)
    eye = jnp.where(ri == ci, 1.0, 0.0).astype(F32)

    ys = []
    for h in range(N_HEADS):
        sl = slice(h * HEAD, (h + 1) * HEAD)
        ad_h, rd_h, bi_h, ki_h, v_h = ad[:, sl], rd[:, sl], bi[:, sl], ki[:, sl], vb[:, sl]
        lmat = jnp.where(strict, _dot_nt(ad_h, bi_h), 0.0)
        lak = jnp.where(strict, _dot_nt(ad_h, ki_h), 0.0)
        mrb = jnp.where(incl, _dot_nt(rd_h, bi_h), 0.0)
        mrk = jnp.where(incl, _dot_nt(rd_h, ki_h), 0.0)
        x = eye + lmat
        lpow = lmat
        for _ in range(5):
            lpow = _dot3(lpow, lpow)
            x = x + _dot3(x, lpow)
        s_h = state_ref[h]
        s_b = s_h.astype(BF16)
        w_ = _dot_nt(ad_h, s_b) + _dot(lak.astype(BF16), v_h)
        u = _dot3(x, w_)
        u_b = u.astype(BF16)
        y_h = _dot_nt(rd_h, s_b) + _dot(mrb.astype(BF16), u_b) + _dot(mrk.astype(BF16), v_h)
        state_ref[h] = s_h * dec[:, sl] + _dot_tn(u_b, bip[:, sl]) + _dot_tn(v_h, kip[:, sl])
        ys.append(y_h)
    y = jnp.concatenate(ys, axis=-1)

    mean = head_sum(y) * (1.0 / HEAD)
    dlt = y - mean
    var = head_sum(dlt * dlt) * (1.0 / HEAD)
    y = dlt * lax.rsqrt(var + GN_EPS) * gn_w + gn_b
    y = y + head_sum(r * k2 * r_k) * v
    y_ref[...] = y * g


def _rwkv(z, batch, seq, tmix, tml, vecs, wd, au, gu, bd):
    c = RWKV_CHUNK
    nc = seq // c
    const = lambda shape: pl.BlockSpec(shape, lambda b, j: (0,) * len(shape))
    return pl.pallas_call(
        _rwkv_kernel,
        grid=(batch, nc),
        in_specs=[
            pl.BlockSpec((c, D_MODEL), lambda b, j: (b * nc + j, 0)),
            pl.BlockSpec((c, D_MODEL), lambda b, j: (b * nc + j, 1)),
            pl.BlockSpec((c, D_MODEL), lambda b, j: (b * nc + j, 2)),
            pl.BlockSpec((c, LORA_PAD), lambda b, j: (b * nc + j, LORA_BLOCK)),
            const((3, D_MODEL)),
            const((1, LORA_PAD)),
            const((8, D_MODEL)),
            const((LORA_PAD, D_MODEL)),
            const((LORA_PAD, D_MODEL)),
            const((LORA_PAD, D_MODEL)),
            const((D_MODEL, D_MODEL)),
        ],
        out_specs=pl.BlockSpec((c, D_MODEL), lambda b, j: (b * nc + j, 0)),
        out_shape=jax.ShapeDtypeStruct((batch * seq, D_MODEL), F32),
        scratch_shapes=[
            pltpu.VMEM((N_HEADS, HEAD, HEAD), F32),
            pltpu.VMEM((1, 3 * D_MODEL + LORA_PAD), F32),
        ],
        compiler_params=pltpu.CompilerParams(
            dimension_semantics=("parallel", "arbitrary"), vmem_limit_bytes=VMEM_LIMIT),
        name="rwkv7",
    )(z, z, z, z, tmix, tml, vecs, wd, au, gu, bd)


def _mix_kernel(x_ref, zu_ref, zv_ref, zga_ref, zgb_ref, ya_ref, lng_ref, lnb_ref, ws_ref, bs_ref,
                bgate_ref, wout_ref, g2_ref, wr_ref, br_ref,
                x2_ref, topi_ref, gate_ref, rank_ref, counts_ref, carry_ref, *, tm):
    @pl.when(pl.program_id(0) == 0)
    def _():
        carry_ref[...] = jnp.zeros_like(carry_ref)

    u = _gelu(zu_ref[...])
    vf = _gelu(zv_ref[...])
    mean = jnp.mean(vf, axis=-1, keepdims=True)
    dlt = vf - mean
    var = jnp.mean(dlt * dlt, axis=-1, keepdims=True)
    vn = (dlt * lax.rsqrt(var + LN_EPS) * lng_ref[...] + lnb_ref[...]).astype(BF16)

    p = GMLP_CHUNK
    ri = lax.broadcasted_iota(jnp.int32, (p, p), 0)
    ci = lax.broadcasted_iota(jnp.int32, (p, p), 1)
    ws = [jnp.where(ri >= ci, ws_ref[gi], 0.0).astype(BF16) for gi in range(N_GROUPS)]
    rows = []
    for cc in range(tm // p):
        cols = [_dot(ws[gi], vn[cc * p:(cc + 1) * p, gi * p:(gi + 1) * p]) for gi in range(N_GROUPS)]
        rows.append(jnp.concatenate(cols, axis=-1) + bs_ref[...])
    s = jnp.concatenate(rows, axis=0) if len(rows) > 1 else rows[0]
    y_b = u * s

    g_a = jax.nn.sigmoid(zga_ref[...] + bgate_ref[:, 0:D_MODEL])
    g_b = jax.nn.sigmoid(zgb_ref[...] + bgate_ref[:, D_MODEL:2 * D_MODEL])
    merged = g_a * ya_ref[...] + g_b * y_b
    x2 = x_ref[...] + _dot(merged.astype(BF16), wout_ref[...])
    x2_ref[...] = x2

    ms = jnp.mean(x2 * x2, axis=-1, keepdims=True)
    h2 = x2 * lax.rsqrt(ms + RMS_EPS) * g2_ref[...]

    hh, hl = _split(h2)
    wh, wl = _split(wr_ref[...])
    logits = _dot_nt(wh, hh) + _dot_nt(wh, hl) + _dot_nt(wl, hh) + br_ref[...]

    e_iota = lax.broadcasted_iota(jnp.int32, (N_EXPERTS, tm), 0).astype(F32)
    work = logits
    top_v, top_i, onehots = [], [], []
    for _ in range(TOP_K):
        m = jnp.max(work, axis=0, keepdims=True)
        idx = jnp.min(jnp.where(work == m, e_iota, float(N_EXPERTS)), axis=0, keepdims=True)
        oh = e_iota == idx
        top_v.append(m)
        top_i.append(idx)
        onehots.append(oh)
        work = jnp.where(oh, -jnp.inf, work)
    exps = [jnp.exp(tv - top_v[0]) for tv in top_v]
    denom = exps[0] + exps[1] + exps[2] + exps[3]

    member = jnp.zeros((N_EXPERTS, tm), F32)
    for oh in onehots:
        member = member + jnp.where(oh, 1.0, 0.0)
    ti = lax.broadcasted_iota(jnp.int32, (tm, tm), 0)
    tj = lax.broadcasted_iota(jnp.int32, (tm, tm), 1)
    before = jnp.where(ti < tj, 1.0, 0.0).astype(BF16)
    base = carry_ref[:, 0:1] + _dot(member.astype(BF16), before)
    for kk_ in range(TOP_K):
        topi_ref[kk_:kk_ + 1, :] = top_i[kk_].astype(jnp.int32)
        gate_ref[kk_:kk_ + 1, :] = exps[kk_] / denom
        rank = jnp.sum(jnp.where(onehots[kk_], base, 0.0), axis=0, keepdims=True)
        rank_ref[kk_:kk_ + 1, :] = rank.astype(jnp.int32)
    carry_ref[...] = carry_ref[...] + jnp.sum(member, axis=1, keepdims=True)
    counts_ref[...] = carry_ref[...]


def _mix(x2d, z, ya, lng, lnb, ws, bs, bgate, wout, g2, wr_t, br):
    t = x2d.shape[0]
    tm = min(256, t)
    const = lambda shape: pl.BlockSpec(shape, lambda i: (0,) * len(shape))
    tok = lambda col: pl.BlockSpec((tm, D_MODEL), lambda i, col=col: (i, col))
    kt = pl.BlockSpec((TOP_K, tm), lambda i: (0, i))
    return pl.pallas_call(
        functools.partial(_mix_kernel, tm=tm),
        grid=(t // tm,),
        in_specs=[
            tok(0), tok(3), tok(4), tok(5), tok(6), tok(0),
            const((1, D_MODEL)), const((1, D_MODEL)),
            const((N_GROUPS, GMLP_CHUNK, GMLP_CHUNK)), const((GMLP_CHUNK, D_MODEL)),
            const((1, 2 * D_MODEL)), const((D_MODEL, D_MODEL)), const((1, D_MODEL)),
            const((N_EXPERTS, D_MODEL)), const((N_EXPERTS, 1)),
        ],
        out_specs=[tok(0), kt, kt, kt, const((N_EXPERTS, 128))],
        out_shape=[
            jax.ShapeDtypeStruct((t, D_MODEL), F32),
            jax.ShapeDtypeStruct((TOP_K, t), jnp.int32),
            jax.ShapeDtypeStruct((TOP_K, t), F32),
            jax.ShapeDtypeStruct((TOP_K, t), jnp.int32),
            jax.ShapeDtypeStruct((N_EXPERTS, 128), F32),
        ],
        scratch_shapes=[pltpu.VMEM((N_EXPERTS, 128), F32)],
        compiler_params=pltpu.CompilerParams(
            dimension_semantics=("arbitrary",), vmem_limit_bytes=VMEM_LIMIT),
        name="gmlp_merge_router",
    )(x2d, z, z, z, z, ya, lng, lnb, ws, bs, bgate, wout, g2, wr_t, br)


def _dispatch_kernel(dest_ref, x2_ref, g2_ref, xs_in_ref, xs_ref, h_scr, sem, *, tm):
    del xs_in_ref
    x2 = x2_ref[...]
    ms = jnp.mean(x2 * x2, axis=-1, keepdims=True)
    h_scr[...] = x2 * lax.rsqrt(ms + RMS_EPS) * g2_ref[...]

    def issue(i, carry):
        t = lax.rem(i, tm)
        d = dest_ref[0, 0, i]
        pltpu.make_async_copy(h_scr.at[pl.ds(t, 1)], xs_ref.at[pl.ds(d, 1)], sem).start()
        return carry

    lax.fori_loop(0, TOP_K * tm, issue, 0)
    for _ in range(TOP_K):
        pltpu.make_async_copy(h_scr, xs_ref.at[pl.ds(0, tm)], sem).wait()


def _dispatch(dest_tiles, x2, g2, xs_init, tm):
    t = x2.shape[0]
    return pl.pallas_call(
        functools.partial(_dispatch_kernel, tm=tm),
        grid=(t // tm,),
        in_specs=[
            pl.BlockSpec((1, 1, TOP_K * tm), lambda i: (i, 0, 0), memory_space=pltpu.SMEM),
            pl.BlockSpec((tm, D_MODEL), lambda i: (i, 0)),
            pl.BlockSpec((1, D_MODEL), lambda i: (0, 0)),
            pl.BlockSpec(memory_space=pl.ANY),
        ],
        out_specs=pl.BlockSpec(memory_space=pl.ANY),
        out_shape=jax.ShapeDtypeStruct(xs_init.shape, F32),
        scratch_shapes=[pltpu.VMEM((tm, D_MODEL), F32), pltpu.SemaphoreType.DMA(())],
        input_output_aliases={3: 0},
        compiler_params=pltpu.CompilerParams(
            dimension_semantics=("arbitrary",), vmem_limit_bytes=VMEM_LIMIT),
        name="moe_dispatch",
    )(dest_tiles, x2, g2, xs_init)


def _expert_kernel(be_ref, nu_ref, xs_ref, wg_ref, wu_ref, bg_ref, bu_ref, wd_ref, bd_ref, ys_ref):
    del be_ref

    @pl.when(pl.program_id(0) < nu_ref[0])
    def _():
        x = xs_ref[...].astype(BF16)
        gate = jnp.minimum(_dot(x, wg_ref[0]) + bg_ref[0], SWIGLU_LIMIT)
        up = jnp.clip(_dot(x, wu_ref[0]) + bu_ref[0], -SWIGLU_LIMIT, SWIGLU_LIMIT)
        act = gate * jax.nn.sigmoid(SWIGLU_ALPHA * gate) * (up + 1.0)
        ys_ref[...] = _dot(act.astype(BF16), wd_ref[0]) + bd_ref[0]

    @pl.when(pl.program_id(0) >= nu_ref[0])
    def _():
        ys_ref[...] = jnp.zeros_like(ys_ref)


def _experts(block_expert, n_used, xs, wg, wu, bg, bu, wd, bd):
    n_slots = xs.shape[0]
    n_blocks = n_slots // MOE_ROWS
    rows = lambda i, be, nu: (jnp.minimum(i, nu[0] - 1), 0)
    per_e = lambda i, be, nu: (be[i], 0, 0)
    f = D_MODEL
    return pl.pallas_call(
        _expert_kernel,
        grid_spec=pltpu.PrefetchScalarGridSpec(
            num_scalar_prefetch=2,
            grid=(n_blocks,),
            in_specs=[
                pl.BlockSpec((MOE_ROWS, D_MODEL), rows),
                pl.BlockSpec((1, D_MODEL, f), per_e),
                pl.BlockSpec((1, D_MODEL, f), per_e),
                pl.BlockSpec((1, 1, f), per_e),
                pl.BlockSpec((1, 1, f), per_e),
                pl.BlockSpec((1, f, D_MODEL), per_e),
                pl.BlockSpec((1, 1, D_MODEL), per_e),
            ],
            out_specs=pl.BlockSpec((MOE_ROWS, D_MODEL), lambda i, be, nu: (i, 0)),
        ),
        out_shape=jax.ShapeDtypeStruct((n_slots, D_MODEL), F32),
        compiler_params=pltpu.CompilerParams(
            dimension_semantics=("arbitrary",), vmem_limit_bytes=VMEM_LIMIT),
        name="moe_experts",
    )(block_expert, n_used, xs, wg, wu, bg, bu, wd, bd)


def _combine_kernel(dest_ref, x2_ref, gates_ref, gf_ref, ys_ref, o_ref, buf, sem, *, tm):
    def issue(i, carry):
        kk_ = i // tm
        t = lax.rem(i, tm)
        d = dest_ref[0, 0, i]
        pltpu.make_async_copy(ys_ref.at[pl.ds(d, 1)], buf.at[kk_, pl.ds(t, 1)], sem).start()
        return carry

    lax.fori_loop(0, TOP_K * tm, issue, 0)
    for kk_ in range(TOP_K):
        pltpu.make_async_copy(ys_ref.at[pl.ds(0, tm)], buf.at[kk_], sem).wait()

    acc = x2_ref[...]
    for kk_ in range(TOP_K):
        acc = acc + gates_ref[:, kk_:kk_ + 1] * buf[kk_]
    ms = jnp.mean(acc * acc, axis=-1, keepdims=True)
    o_ref[...] = acc * lax.rsqrt(ms + RMS_EPS) * gf_ref[...]


def _combine(dest_tiles, x2, gates_tk, gf, ys, tm):
    t = x2.shape[0]
    return pl.pallas_call(
        functools.partial(_combine_kernel, tm=tm),
        grid=(t // tm,),
        in_specs=[
            pl.BlockSpec((1, 1, TOP_K * tm), lambda i: (i, 0, 0), memory_space=pltpu.SMEM),
            pl.BlockSpec((tm, D_MODEL), lambda i: (i, 0)),
            pl.BlockSpec((tm, TOP_K), lambda i: (i, 0)),
            pl.BlockSpec((1, D_MODEL), lambda i: (0, 0)),
            pl.BlockSpec(memory_space=pl.ANY),
        ],
        out_specs=pl.BlockSpec((tm, D_MODEL), lambda i: (i, 0)),
        out_shape=jax.ShapeDtypeStruct((t, D_MODEL), F32),
        scratch_shapes=[pltpu.VMEM((TOP_K, tm, D_MODEL), F32), pltpu.SemaphoreType.DMA(())],
        compiler_params=pltpu.CompilerParams(
            dimension_semantics=("arbitrary",), vmem_limit_bytes=VMEM_LIMIT),
        name="moe_combine",
    )(dest_tiles, x2, gates_tk, gf, ys)


def _pad_rows(w, lo, total):
    return jnp.zeros((total, w.shape[1]), w.dtype).at[lo:lo + w.shape[0]].set(w)


def _layer(x2d, batch, seq, norm1_g, w_in, b_gate, time_mix, w_decay_up, w0, a_up, a0, g_up, k_k, k_a,
           r_k, gn_w, gn_b, ln_v_g, ln_v_b, w_spatial, b_spatial, w_out, norm2_g, w_router, b_router,
           w_gate_up, b_gate_up, w_down, b_down, norm_out_g):
    t = batch * seq
    row = lambda vec: vec.reshape(1, -1).astype(F32)

    w_new = jnp.concatenate(
        [w_in[:, :3 * D_MODEL], w_in[:, COLS_A:], w_in[:, 3 * D_MODEL:COLS_A],
         jnp.zeros((D_MODEL, LORA_PAD - LORA), w_in.dtype)], axis=1).astype(BF16)
    z = _inproj(x2d, row(norm1_g), w_new)

    tmix = time_mix[:3 * D_MODEL].reshape(3, D_MODEL)
    tml = jnp.pad(time_mix[3 * D_MODEL:], (0, LORA_PAD - LORA)).reshape(1, LORA_PAD)
    vecs = jnp.stack([w0, a0, k_k, k_a, r_k, gn_w, gn_b, jnp.zeros_like(w0)]).astype(F32)
    wd = _pad_rows(w_decay_up, 0, LORA_PAD).astype(BF16)
    au = _pad_rows(a_up, D_DECAY_LORA, LORA_PAD).astype(BF16)
    gu = _pad_rows(g_up, D_DECAY_LORA + D_AAA_LORA, LORA_PAD).astype(BF16)
    head_id = jnp.arange(D_MODEL) // HEAD
    bd = (head_id[:, None] == head_id[None, :]).astype(BF16)
    ya = _rwkv(z, batch, seq, tmix, tml, vecs, wd, au, gu, bd)

    bs_full = jnp.repeat(b_spatial.T.astype(F32), GMLP_CHUNK, axis=1)
    x2, top_i, gates, rank, counts = _mix(
        x2d, z, ya, row(ln_v_g), row(ln_v_b), w_spatial.astype(F32), bs_full, row(b_gate),
        w_out.astype(BF16), row(norm2_g), w_router.T.astype(F32), b_router.reshape(-1, 1).astype(F32))

    counts = counts[:, 0].astype(jnp.int32)
    padded = (counts + MOE_ROWS - 1) // MOE_ROWS * MOE_ROWS
    pad_end = jnp.cumsum(padded)
    pad_start = pad_end - padded
    n_blocks = (t * TOP_K) // MOE_ROWS + N_EXPERTS
    n_used = (pad_end[-1] // MOE_ROWS).astype(jnp.int32)
    blk = jnp.minimum(jnp.arange(n_blocks, dtype=jnp.int32), n_used - 1)
    block_expert = jnp.minimum(
        jnp.searchsorted(pad_end, blk * MOE_ROWS, side='right'), N_EXPERTS - 1).astype(jnp.int32)
    dest = pad_start[top_i] + rank
    tmd = min(256, t)
    dest_tiles = dest.reshape(TOP_K, t // tmd, tmd).transpose(1, 0, 2).reshape(t // tmd, 1, TOP_K * tmd)

    xs = _dispatch(dest_tiles, x2, row(norm2_g), jnp.zeros((n_blocks * MOE_ROWS, D_MODEL), F32), tmd)
    wg = w_gate_up[:, :, 0::2].astype(BF16)
    wu = w_gate_up[:, :, 1::2].astype(BF16)
    bg = b_gate_up[:, None, 0::2].astype(F32)
    bu = b_gate_up[:, None, 1::2].astype(F32)
    ys = _experts(block_expert, n_used.reshape(1), xs, wg, wu, bg, bu, w_down.astype(BF16),
                  b_down[:, None, :].astype(F32))
    return _combine(dest_tiles, x2, gates.T, row(norm_out_g), ys, tmd)


def kernel(x, norm1_g, w_in, b_gate, time_mix, w_decay_up, w0, a_up, a0, g_up, k_k, k_a, r_k, gn_w, gn_b,
           ln_v_g, ln_v_b, w_spatial, b_spatial, w_out, norm2_g, w_router, b_router, w_gate_up, b_gate_up,
           w_down, b_down, norm_f_g):
    batch, seq, d = x.shape
    assert d == D_MODEL and seq % GMLP_CHUNK == 0 and norm1_g.shape[0] == 1
    out = _layer(x.reshape(batch * seq, d), batch, seq, norm1_g[0], w_in[0], b_gate[0], time_mix[0],
                 w_decay_up[0], w0[0], a_up[0], a0[0], g_up[0], k_k[0], k_a[0], r_k[0], gn_w[0], gn_b[0],
                 ln_v_g[0], ln_v_b[0], w_spatial[0], b_spatial[0], w_out[0], norm2_g[0], w_router[0],
                 b_router[0], w_gate_up[0], b_gate_up[0], w_down[0], b_down[0], norm_f_g)
    return out.reshape(batch, seq, d)
```
